```python
import math
import jax, jax.numpy as jnp
from jax import lax
import numpy as np

D_MODEL = 2048
BATCH = 2
SEQ = 8192
DEPTH = 1

GRID_W = 64
CTX_LEN = 256
ROWS_PER_CHUNK = 2
CHUNK = ROWS_PER_CHUNK * GRID_W
D_MIX = D_MODEL
W_A = D_MIX // 2
HEAD_DIM_A = 128
N_HEADS_A = W_A // HEAD_DIM_A
W_B = D_MIX - W_A
S5_CH = 16
S5_GROUPS = W_B // S5_CH
S5_STATE = 64
N_DIR = 2
IN_COLS = 3 * W_A + 2 * W_B
ALPHA = (2.0 * DEPTH) ** 0.25
OUT_INIT_SCALE = (8.0 * DEPTH) ** -0.25
LN_EPS = 1e-6
F32 = jnp.float32

kernel_name = "hybrid_gmlp_s5_parallel_heads_deepnorm"


def _layer_norm(x):
    x32 = x.astype(F32)
    mu = jnp.mean(x32, axis=-1, keepdims=True)
    var = jnp.mean(jnp.square(x32 - mu), axis=-1, keepdims=True)
    return ((x32 - mu) * lax.rsqrt(var + LN_EPS)).astype(x.dtype)


def _modulate(x, shift, scale):
    return _layer_norm(x) * (1 + scale[:, None, :]) + shift[:, None, :]


def _chunk_mlp(uv, n_chunks, ln_g, ln_b, w_s, b_s):
    bsz = uv.shape[0]
    u, v = jnp.split(jax.nn.gelu(uv, approximate=False), 2, axis=-1)
    v = _layer_norm(v) * ln_g + ln_b
    v = v.reshape(bsz, n_chunks, CHUNK, N_HEADS_A, HEAD_DIM_A)
    mixed = jnp.einsum("hpq,bnqhd->bnphd", w_s, v) + b_s.T[None, None, :, :, None]
    return u * mixed.reshape(bsz, n_chunks * CHUNK, W_A)


def _s5_discretize(lam_re, lam_im, log_step, b_re, b_im):
    step = jnp.exp(log_step.astype(F32))[:, None]
    lr, li = lam_re.astype(F32), lam_im.astype(F32)
    dr, di = lr * step, li * step
    mag = jnp.exp(dr)
    ab_re, ab_im = mag * jnp.cos(di), mag * jnp.sin(di)
    den = lr * lr + li * li
    nr, ni = ab_re - 1.0, ab_im
    f_re = (nr * lr + ni * li) / den
    f_im = (ni * lr - nr * li) / den
    br, bi = b_re.astype(F32), b_im.astype(F32)
    bb_re = f_re[..., None] * br - f_im[..., None] * bi
    bb_im = f_re[..., None] * bi + f_im[..., None] * br
    return ab_re, ab_im, bb_re, bb_im


def _ssm_combine(e1, e2):
    a1r, a1i, b1r, b1i = e1
    a2r, a2i, b2r, b2i = e2
    ar = a1r * a2r - a1i * a2i
    ai = a1r * a2i + a1i * a2r
    br = a2r * b1r - a2i * b1i + b2r
    bi = a2r * b1i + a2i * b1r + b2i
    return ar, ai, br, bi


def _s5_scan(u, ab_re, ab_im, bb_re, bb_im, h0):
    bu_re = jnp.einsum("lbgc,gpc->lbgp", u, bb_re)
    bu_im = jnp.einsum("lbgc,gpc->lbgp", u, bb_im)
    if h0 is not None:
        h0_re, h0_im = h0
        bu_re = bu_re.at[0].add(ab_re * h0_re - ab_im * h0_im)
        bu_im = bu_im.at[0].add(ab_re * h0_im + ab_im * h0_re)
    length = u.shape[0]
    a_re = jnp.broadcast_to(ab_re, (length, 1) + ab_re.shape)
    a_im = jnp.broadcast_to(ab_im, (length, 1) + ab_im.shape)
    _, _, h_re, h_im = lax.associative_scan(_ssm_combine, (a_re, a_im, bu_re, bu_im), axis=0)
    return h_re, h_im


def _s5_readout(h_re, h_im, c_re, c_im):
    return (jnp.einsum("lbgp,gcp->lbgc", h_re, c_re.astype(F32))
            - jnp.einsum("lbgp,gcp->lbgc", h_im, c_im.astype(F32)))


def _s5_branch(u_lat, u_ctx, lam_re, lam_im, log_step, b_re, b_im, c_re, c_im,
               d_skip, w_glu, b_glu, with_ctx_out):
    def to_lbgc(u):
        bsz, length, _ = u.shape
        return u.astype(F32).reshape(bsz, length, S5_GROUPS, S5_CH).transpose(1, 0, 2, 3)

    ul, uc = to_lbgc(u_lat), to_lbgc(u_ctx)
    ys_lat, ys_ctx = [], []
    for d in range(N_DIR):
        ab_re, ab_im, bb_re, bb_im = _s5_discretize(lam_re[d], lam_im[d], log_step[d], b_re[d], b_im[d])
        ucd, uld = (uc, ul) if d == 0 else (uc[::-1], ul[::-1])
        hc_re, hc_im = _s5_scan(ucd, ab_re, ab_im, bb_re, bb_im, None)
        hl_re, hl_im = _s5_scan(uld, ab_re, ab_im, bb_re, bb_im, (hc_re[-1], hc_im[-1]))
        yl = _s5_readout(hl_re, hl_im, c_re[d], c_im[d])
        ys_lat.append(yl if d == 0 else yl[::-1])
        if with_ctx_out:
            yc = _s5_readout(hc_re, hc_im, c_re[d], c_im[d])
            ys_ctx.append(yc if d == 0 else yc[::-1])

    d_grp = d_skip.astype(F32).reshape(S5_GROUPS, S5_CH)

    def finish(y, u, dtype):
        y = y + d_grp * u
        length, bsz = y.shape[0], y.shape[1]
        y = y.transpose(1, 0, 2, 3).reshape(bsz, length, W_B)
        y = jax.nn.gelu(y, approximate=False).astype(dtype)
        return y * jax.nn.sigmoid(y @ w_glu + b_glu)

    y_lat = finish(ys_lat[0] + ys_lat[1], ul, u_lat.dtype)
    y_ctx = finish(ys_ctx[0] + ys_ctx[1], uc, u_ctx.dtype) if with_ctx_out else None
    return y_lat, y_ctx


def setup_inputs(seed: int = 0) -> dict:
    key = jax.random.key(seed)
    ks = jax.random.split(key, 24)
    nrm = jax.random.normal
    D = D_MODEL
    x = nrm(ks[0], (BATCH, SEQ, D), F32)
    c = nrm(ks[1], (BATCH, D), F32)
    ctx = nrm(ks[2], (BATCH, CTX_LEN, D), F32)
    c_ctx = nrm(ks[3], (D,), F32)
    w_ada = nrm(ks[4], (DEPTH, D, 3 * D), F32) * (D ** -0.5) * 0.5
    b_ada = 0.02 * nrm(ks[5], (DEPTH, 3 * D), F32) + jnp.concatenate(
        [jnp.zeros((2 * D,), F32), jnp.ones((D,), F32)])[None]
    w_in = nrm(ks[6], (DEPTH, D, IN_COLS), F32) * (D ** -0.5)
    sgu_ln_g = 1.0 + 0.02 * nrm(ks[7], (DEPTH, W_A), F32)
    sgu_ln_b = 0.02 * nrm(ks[8], (DEPTH, W_A), F32)
    w_spatial = nrm(ks[9], (DEPTH, N_HEADS_A, CHUNK, CHUNK), F32) * (CHUNK ** -0.5)
    b_spatial = 1.0 + 0.02 * nrm(ks[10], (DEPTH, N_HEADS_A, CHUNK), F32)
    n_idx = jnp.arange(S5_STATE, dtype=F32)
    s5_shape = (DEPTH, N_DIR, S5_GROUPS, S5_STATE)
    s5_lam_re = -0.5 + 0.01 * nrm(ks[11], s5_shape, F32)
    s5_lam_im = math.pi * n_idx + 0.01 * nrm(ks[12], s5_shape, F32)
    s5_log_step = jax.random.uniform(ks[13], (DEPTH, N_DIR, S5_GROUPS), F32,
                                     minval=math.log(1e-3), maxval=math.log(1e-1))
    b_shape = (DEPTH, N_DIR, S5_GROUPS, S5_STATE, S5_CH)
    s5_b_re = nrm(ks[14], b_shape, F32) * ((2 * S5_CH) ** -0.5)
    s5_b_im = nrm(ks[15], b_shape, F32) * ((2 * S5_CH) ** -0.5)
    c_shape = (DEPTH, N_DIR, S5_GROUPS, S5_CH, S5_STATE)
    s5_c_re = nrm(ks[16], c_shape, F32) * (0.5 ** 0.5)
    s5_c_im = nrm(ks[17], c_shape, F32) * (0.5 ** 0.5)
    s5_d = nrm(ks[18], (DEPTH, W_B), F32)
    w_glu = nrm(ks[19], (DEPTH, W_B, W_B), F32) * (W_B ** -0.5)
    b_glu = 0.02 * nrm(ks[20], (DEPTH, W_B), F32)
    w_out = nrm(ks[21], (DEPTH, D_MIX, D), F32) * (D_MIX ** -0.5) * OUT_INIT_SCALE
    ln_g = 1.0 + 0.02 * nrm(ks[22], (DEPTH, D), F32)
    ln_b = 0.02 * nrm(ks[23], (DEPTH, D), F32)
    return {"x": x, "c": c, "ctx": ctx, "c_ctx": c_ctx,
            "w_ada": w_ada, "b_ada": b_ada, "w_in": w_in,
            "sgu_ln_g": sgu_ln_g, "sgu_ln_b": sgu_ln_b,
            "w_spatial": w_spatial, "b_spatial": b_spatial,
            "s5_lam_re": s5_lam_re, "s5_lam_im": s5_lam_im, "s5_log_step": s5_log_step,
            "s5_b_re": s5_b_re, "s5_b_im": s5_b_im, "s5_c_re": s5_c_re, "s5_c_im": s5_c_im,
            "s5_d": s5_d, "w_glu": w_glu, "b_glu": b_glu, "w_out": w_out,
            "ln_g": ln_g, "ln_b": ln_b}


def reference(x, c, ctx, c_ctx, w_ada, b_ada, w_in, sgu_ln_g, sgu_ln_b, w_spatial, b_spatial,
              s5_lam_re, s5_lam_im, s5_log_step, s5_b_re, s5_b_im, s5_c_re, s5_c_im,
              s5_d, w_glu, b_glu, w_out, ln_g, ln_b):
    rows = x.shape[1] // GRID_W
    n_chunks_lat = rows // ROWS_PER_CHUNK
    n_chunks_ctx = ctx.shape[1] // CHUNK
    col_b0, col_b1 = 3 * W_A, 3 * W_A + W_B
    for i in range(DEPTH):
        update_ctx = i < DEPTH - 1
        mod_x = jax.nn.silu(c) @ w_ada[i] + b_ada[i]
        mod_c = (jax.nn.silu(c_ctx) @ w_ada[i] + b_ada[i])[None]
        shift_x, scale_x, gate_x = jnp.split(mod_x, 3, axis=-1)
        shift_c, scale_c, gate_c = jnp.split(mod_c, 3, axis=-1)

        proj_x = _modulate(x, shift_x, scale_x) @ w_in[i]
        hc = _modulate(ctx, shift_c, scale_c)
        ub_c = hc @ w_in[i][:, col_b0:col_b1]
        uv_x, za_x = proj_x[..., :2 * W_A], proj_x[..., 2 * W_A:col_b0]
        ub_x, zb_x = proj_x[..., col_b0:col_b1], proj_x[..., col_b1:]

        ya_x = _chunk_mlp(uv_x, n_chunks_lat, sgu_ln_g[i], sgu_ln_b[i],
                          w_spatial[i], b_spatial[i]) * jax.nn.silu(za_x)
        yb_x, yb_c = _s5_branch(ub_x, ub_c, s5_lam_re[i], s5_lam_im[i], s5_log_step[i],
                                s5_b_re[i], s5_b_im[i], s5_c_re[i], s5_c_im[i],
                                s5_d[i], w_glu[i], b_glu[i], update_ctx)
        yb_x = yb_x * jax.nn.silu(zb_x)

        out_x = jnp.concatenate([ya_x, yb_x], axis=-1) @ w_out[i]
        x_new = _layer_norm(ALPHA * x + gate_x[:, None, :] * out_x) * ln_g[i] + ln_b[i]

        if update_ctx:
            rest_c = hc @ w_in[i]
            ya_c = _chunk_mlp(rest_c[..., :2 * W_A], n_chunks_ctx, sgu_ln_g[i], sgu_ln_b[i],
                              w_spatial[i], b_spatial[i]) * jax.nn.silu(rest_c[..., 2 * W_A:col_b0])
            yb_c = yb_c * jax.nn.silu(rest_c[..., col_b1:])
            out_c = jnp.concatenate([ya_c, yb_c], axis=-1) @ w_out[i]
            ctx = _layer_norm(ALPHA * ctx + gate_c[:, None, :] * out_c) * ln_g[i] + ln_b[i]
        x = x_new
    return x
```

```python
import functools
import math

import jax
import jax.numpy as jnp
from jax import lax
from jax.experimental import pallas as pl
from jax.experimental.pallas import tpu as pltpu

F32 = jnp.float32
BF16 = jnp.bfloat16

D_MODEL = 2048
W_A = 1024
W_B = 1024
N_HEADS_A = 8
HEAD_DIM_A = 128
GMLP_CHUNK = 128
S5_GROUPS = 64
S5_CH = 16
S5_STATE = 64
S5_T = 16
S5_ROWS = S5_T * S5_CH
S5_LANES = 2 * S5_STATE
GROUPS_PER_STEP = 8
LN_EPS = 1e-6
ALPHA = 2.0 ** 0.25
SQRT_HALF = math.sqrt(0.5)
VMEM_LIMIT = 56 * 1024 * 1024


def _sigmoid(x):
    return 1.0 / (1.0 + jnp.exp(-x))


def _gelu(x):
    return 0.5 * x * (1.0 + lax.erf(x * SQRT_HALF))


def _layer_norm(x):
    mu = jnp.mean(x, axis=-1, keepdims=True)
    xc = x - mu
    var = jnp.mean(xc * xc, axis=-1, keepdims=True)
    return xc * lax.rsqrt(var + LN_EPS)


def _mm(a, b):
    return jnp.dot(a, b, preferred_element_type=F32)


def _mm_nt(a, b):
    return lax.dot_general(a, b, (((1,), (1,)), ((), ())), preferred_element_type=F32)


def _params(*sem):
    return pltpu.CompilerParams(dimension_semantics=sem, vmem_limit_bytes=VMEM_LIMIT)


def _mods_kernel(c_ref, w_ref, b_ref, o_ref):
    c = c_ref[...]
    a = c * _sigmoid(c)
    o_ref[...] = jnp.dot(a, w_ref[...], precision=lax.Precision.HIGHEST,
                         preferred_element_type=F32) + b_ref[...]


def _mods(c_rows, w_ada, b_ada):
    n_out = w_ada.shape[1]
    tn = 512
    return pl.pallas_call(
        _mods_kernel,
        grid=(n_out // tn,),
        in_specs=[pl.BlockSpec((8, D_MODEL), lambda i: (0, 0)),
                  pl.BlockSpec((D_MODEL, tn), lambda i: (0, i)),
                  pl.BlockSpec((1, tn), lambda i: (0, i))],
        out_specs=pl.BlockSpec((8, tn), lambda i: (0, i)),
        out_shape=jax.ShapeDtypeStruct((8, n_out), F32),
        compiler_params=_params("arbitrary"),
        name="mods",
    )(c_rows, w_ada, b_ada)


def _s5_ops_kernel(lre_ref, lim_ref, lstep_ref, cre_ref, cim_ref, bre_ref, bim_ref,
                   p_ref, m_ref, q_ref, a_ref):
    lr = lre_ref[0]
    li = lim_ref[0]
    step = jnp.exp(lstep_ref[0])
    dr = lr * step
    di = li * step
    mag = jnp.exp(dr)
    ab_re = mag * jnp.cos(di)
    ab_im = mag * jnp.sin(di)
    den = lr * lr + li * li
    nr = ab_re - 1.0
    ni = ab_im
    f_re = (nr * lr + ni * li) / den
    f_im = (ni * lr - nr * li) / den
    b_re = bre_ref[0]
    b_im = bim_ref[0]
    bb_re = f_re * b_re - f_im * b_im
    bb_im = f_re * b_im + f_im * b_re
    c_re = cre_ref[0]
    c_im = cim_ref[0]

    lane = lax.broadcasted_iota(jnp.int32, (S5_T, S5_LANES), 1)
    pos = lax.broadcasted_iota(jnp.int32, (S5_T, S5_LANES), 0)
    is_fwd = lane < S5_STATE

    def power(e):
        e = e.astype(F32)
        r = jnp.exp(e * dr)
        return r * jnp.cos(e * di), r * jnp.sin(e * di)

    pp_re, pp_im = power(jnp.where(is_fwd, S5_T - 1 - pos, pos))
    pt_re = (pp_re[:, None, :] * bb_re[None] - pp_im[:, None, :] * bb_im[None]).reshape(S5_ROWS, S5_LANES)
    pt_im = (pp_re[:, None, :] * bb_im[None] + pp_im[:, None, :] * bb_re[None]).reshape(S5_ROWS, S5_LANES)
    pt = jnp.concatenate([pt_re, pt_im], axis=1)
    p_ref[0] = pt.T.astype(BF16)

    qq_re, qq_im = power(jnp.where(is_fwd, pos + 1, S5_T - pos))
    q_re = (qq_re[:, None, :] * c_re[None] - qq_im[:, None, :] * c_im[None]).reshape(S5_ROWS, S5_LANES)
    q_im = (qq_re[:, None, :] * c_im[None] + qq_im[:, None, :] * c_re[None]).reshape(S5_ROWS, S5_LANES)
    q_ref[0, :, :S5_LANES] = q_re.astype(BF16)
    q_ref[0, :, S5_LANES:] = (-q_im).astype(BF16)

    e16 = jnp.full((1, S5_LANES), float(S5_T), F32)
    r16 = jnp.exp(e16 * dr)
    a_ref[0, :, :S5_LANES] = r16 * jnp.cos(e16 * di)
    a_ref[0, :, S5_LANES:] = r16 * jnp.sin(e16 * di)

    is_fwd_c = is_fwd[:S5_CH]
    zero = jnp.zeros_like(c_re)
    sel = jnp.concatenate([
        jnp.concatenate([jnp.where(is_fwd_c, c_re, zero), jnp.where(is_fwd_c, -c_im, zero)], axis=1),
        jnp.concatenate([jnp.where(is_fwd_c, zero, c_re), jnp.where(is_fwd_c, zero, -c_im)], axis=1),
    ], axis=0)
    w = lax.dot_general(sel, pt, (((1,), (1,)), ((), ())), precision=lax.Precision.HIGHEST,
                        preferred_element_type=F32)
    w_f = w[:S5_CH]
    w_b = w[S5_CH:]
    col = lax.broadcasted_iota(jnp.int32, (S5_CH, S5_ROWS), 1)
    for i in range(S5_T):
        lo = pltpu.roll(w_f, (S5_CH * (i + 1)) % S5_ROWS, axis=1)
        hi = pltpu.roll(w_b, S5_CH * i, axis=1) if i else w_b
        blk = jnp.where(col < S5_CH * (i + 1), lo, 0.0) + jnp.where(col >= S5_CH * i, hi, 0.0)
        m_ref[0, S5_CH * i:S5_CH * (i + 1), :] = blk.astype(BF16)


def _s5_ops(lam_re, lam_im, log_step, b_re, b_im, c_re, c_im):
    g = S5_GROUPS

    def lanes_dp(v):
        return jnp.transpose(v, (1, 0, 2)).reshape(g, 1, S5_LANES)

    lre = lanes_dp(lam_re)
    lim = lanes_dp(lam_im)
    lstep = lanes_dp(jnp.broadcast_to(log_step[:, :, None], (2, g, S5_STATE)))
    cre = jnp.transpose(c_re, (1, 2, 0, 3)).reshape(g, S5_CH, S5_LANES)
    cim = jnp.transpose(c_im, (1, 2, 0, 3)).reshape(g, S5_CH, S5_LANES)
    bre = jnp.transpose(b_re, (1, 3, 0, 2)).reshape(g, S5_CH, S5_LANES)
    bim = jnp.transpose(b_im, (1, 3, 0, 2)).reshape(g, S5_CH, S5_LANES)
    vec = pl.BlockSpec((1, 1, S5_LANES), lambda i: (i, 0, 0))
    mat = pl.BlockSpec((1, S5_CH, S5_LANES), lambda i: (i, 0, 0))
    op = pl.BlockSpec((1, S5_ROWS, S5_ROWS), lambda i: (i, 0, 0))
    op_shape = jax.ShapeDtypeStruct((g, S5_ROWS, S5_ROWS), BF16)
    return pl.pallas_call(
        _s5_ops_kernel,
        grid=(g,),
        in_specs=[vec, vec, vec, mat, mat, mat, mat],
        out_specs=[op, op, op, pl.BlockSpec((1, 1, 2 * S5_LANES), lambda i: (i, 0, 0))],
        out_shape=[op_shape, op_shape, op_shape,
                   jax.ShapeDtypeStruct((g, 1, 2 * S5_LANES), F32)],
        compiler_params=_params("arbitrary"),
        name="s5_ops",
    )(lre, lim, lstep, cre, cim, bre, bim)


def _branch_a_kernel(x_ref, shift_ref, scale_ref, w_ref, lng_ref, lnb_ref, ws_ref, bsp_ref, ya_ref):
    x = x_ref[0]
    h = _layer_norm(x) * (1.0 + scale_ref[0]) + shift_ref[0]
    p = _mm(h.astype(BF16), w_ref[...])
    u = _gelu(p[:, :W_A])
    v = _gelu(p[:, W_A:2 * W_A])
    z = p[:, 2 * W_A:]
    gate = u * (z * _sigmoid(z))
    vn = (_layer_norm(v) * lng_ref[...] + lnb_ref[...]).astype(BF16)
    tm = x.shape[0]
    for ck in range(tm // GMLP_CHUNK):
        rows = slice(ck * GMLP_CHUNK, (ck + 1) * GMLP_CHUNK)
        for hd in range(N_HEADS_A):
            cols = slice(hd * HEAD_DIM_A, (hd + 1) * HEAD_DIM_A)
            mixed = _mm(ws_ref[hd], vn[rows, cols]) + bsp_ref[:, cols]
            ya_ref[0, rows, cols] = (gate[rows, cols] * mixed).astype(BF16)


def _branch_a(x, shift, scale, w_a, sgu_g, sgu_b, w_s, b_sp, tm=512):
    bsz, seq, _ = x.shape
    const2 = lambda b, i: (0, 0)
    return pl.pallas_call(
        _branch_a_kernel,
        grid=(bsz, seq // tm),
        in_specs=[pl.BlockSpec((1, tm, D_MODEL), lambda b, i: (b, i, 0)),
                  pl.BlockSpec((1, 1, D_MODEL), lambda b, i: (b, 0, 0)),
                  pl.BlockSpec((1, 1, D_MODEL), lambda b, i: (b, 0, 0)),
                  pl.BlockSpec((D_MODEL, 3 * W_A), const2, pipeline_mode=pl.Buffered(1)),
                  pl.BlockSpec((1, W_A), const2),
                  pl.BlockSpec((1, W_A), const2),
                  pl.BlockSpec((N_HEADS_A, GMLP_CHUNK, GMLP_CHUNK), lambda b, i: (0, 0, 0)),
                  pl.BlockSpec((GMLP_CHUNK, W_A), const2)],
        out_specs=pl.BlockSpec((1, tm, W_A), lambda b, i: (b, i, 0)),
        out_shape=jax.ShapeDtypeStruct((bsz, seq, W_A), BF16),
        compiler_params=_params("arbitrary", "arbitrary"),
        name="branch_a",
    )(x, shift, scale, w_a, sgu_g, sgu_b, w_s, b_sp)


def _proj_b_kernel(x_ref, shift_ref, scale_ref, wt_ref, ut_ref, zt_ref):
    x = x_ref[0]
    h = (_layer_norm(x) * (1.0 + scale_ref[0]) + shift_ref[0]).astype(BF16)
    r = _mm_nt(wt_ref[...], h)
    ut_ref[0] = r[:W_B].astype(BF16)
    zt_ref[0] = r[W_B:].astype(BF16)


def _proj_b(x, shift, scale, w_bt):
    bsz, seq, _ = x.shape
    n = seq // S5_T
    xv = x.reshape(bsz, n, S5_T * D_MODEL)
    out_shape = jax.ShapeDtypeStruct((S5_T, W_B, bsz * n), BF16)
    out_spec = pl.BlockSpec((1, W_B, n), lambda b, j: (j, 0, b))
    return pl.pallas_call(
        _proj_b_kernel,
        grid=(bsz, S5_T),
        in_specs=[pl.BlockSpec((1, n, D_MODEL), lambda b, j: (b, 0, j)),
                  pl.BlockSpec((1, 1, D_MODEL), lambda b, j: (b, 0, 0)),
                  pl.BlockSpec((1, 1, D_MODEL), lambda b, j: (b, 0, 0)),
                  pl.BlockSpec((2 * W_B, D_MODEL), lambda b, j: (0, 0), pipeline_mode=pl.Buffered(1))],
        out_specs=[out_spec, out_spec],
        out_shape=[out_shape, out_shape],
        compiler_params=_params("arbitrary", "arbitrary"),
        name="proj_b",
    )(xv, shift, scale, w_bt)


CTX_LANES = 128


def _proj_ctx_kernel(x_ref, shift_ref, scale_ref, wt_ref, ut_ref):
    bsz, nc, _ = x_ref.shape
    x = x_ref[...].reshape(bsz * nc, D_MODEL)
    x = jnp.concatenate([x, jnp.zeros((CTX_LANES - bsz * nc, D_MODEL), F32)], axis=0)
    h = (_layer_norm(x) * (1.0 + scale_ref[...]) + shift_ref[...]).astype(BF16)
    r = _mm_nt(wt_ref[...], h)
    ut_ref[0] = r.astype(BF16)


def _proj_ctx(ctx, shift_c, scale_c, w_ut):
    bsz, lc, _ = ctx.shape
    nc = lc // S5_T
    cv = ctx.reshape(bsz, nc, S5_T * D_MODEL)
    return pl.pallas_call(
        _proj_ctx_kernel,
        grid=(S5_T,),
        in_specs=[pl.BlockSpec((bsz, nc, D_MODEL), lambda j: (0, 0, j)),
                  pl.BlockSpec((1, D_MODEL), lambda j: (0, 0)),
                  pl.BlockSpec((1, D_MODEL), lambda j: (0, 0)),
                  pl.BlockSpec((W_B, D_MODEL), lambda j: (0, 0))],
        out_specs=pl.BlockSpec((1, W_B, CTX_LANES), lambda j: (j, 0, 0)),
        out_shape=jax.ShapeDtypeStruct((S5_T, W_B, CTX_LANES), BF16),
        compiler_params=_params("arbitrary"),
        name="proj_ctx",
    )(cv, shift_c, scale_c, w_ut)


def _ssm_kernel(n_lat, n_ctx, ut_ref, utc_ref, p_ref, m_ref, q_ref, a_ref, d_ref, gt_ref,
                s_sc, sc_sc, h_sc):
    gps = GROUPS_PER_STEP
    bsz = ut_ref.shape[2] // n_lat
    cols = bsz * n_lat

    def group_rows(ref, gl):
        blk = ref[:, S5_CH * gl:S5_CH * (gl + 1), :]
        return blk.reshape(S5_ROWS, blk.shape[2])

    for gl in range(gps):
        st = _mm(p_ref[gl], group_rows(ut_ref, gl)).T
        s_sc[0, pl.ds(gl, cols, stride=gps), :] = st[:, :S5_LANES]
        s_sc[1, pl.ds(gl, cols, stride=gps), :] = st[:, S5_LANES:]
        sc = _mm(p_ref[gl], group_rows(utc_ref, gl)).T
        sc_sc[0, pl.ds(gl, CTX_LANES, stride=gps), :] = sc[:, :S5_LANES]
        sc_sc[1, pl.ds(gl, CTX_LANES, stride=gps), :] = sc[:, S5_LANES:]

    a_re = a_ref[:, :S5_LANES]
    a_im = a_ref[:, S5_LANES:]
    is_fwd = lax.broadcasted_iota(jnp.int32, (gps, S5_LANES), 1) < S5_STATE
    is_bwd = jnp.logical_not(is_fwd)

    def advance(h_re, h_im, src, row_f, row_b):
        s_re = jnp.where(is_fwd, src[0, pl.ds(row_f, gps), :], src[0, pl.ds(row_b, gps), :])
        s_im = jnp.where(is_fwd, src[1, pl.ds(row_f, gps), :], src[1, pl.ds(row_b, gps), :])
        return (a_re * h_re - a_im * h_im + s_re, a_re * h_im + a_im * h_re + s_im)

    def ctx_step(t, carry):
        out = []
        for b in range(bsz):
            row_f = pl.multiple_of((b * n_ctx + t) * gps, gps)
            row_b = pl.multiple_of((b * n_ctx + n_ctx - 1 - t) * gps, gps)
            out.extend(advance(carry[2 * b], carry[2 * b + 1], sc_sc, row_f, row_b))
        return tuple(out)

    def lat_step(t, carry):
        out = []
        for b in range(bsz):
            h_re, h_im = carry[2 * b], carry[2 * b + 1]
            row_f = pl.multiple_of((b * n_lat + t) * gps, gps)
            row_b = pl.multiple_of((b * n_lat + n_lat - 1 - t) * gps, gps)
            pltpu.store(h_sc.at[0, pl.ds(row_f, gps), :], h_re, mask=is_fwd)
            pltpu.store(h_sc.at[1, pl.ds(row_f, gps), :], h_im, mask=is_fwd)
            pltpu.store(h_sc.at[0, pl.ds(row_b, gps), :], h_re, mask=is_bwd)
            pltpu.store(h_sc.at[1, pl.ds(row_b, gps), :], h_im, mask=is_bwd)
            out.extend(advance(h_re, h_im, s_sc, row_f, row_b))
        return tuple(out)

    zero = jnp.zeros((gps, S5_LANES), F32)
    carry = lax.fori_loop(0, n_ctx, ctx_step, (zero,) * (2 * bsz))
    lax.fori_loop(0, n_lat, lat_step, carry)

    for gl in range(gps):
        ht = jnp.concatenate([h_sc[0, pl.ds(gl, cols, stride=gps), :],
                              h_sc[1, pl.ds(gl, cols, stride=gps), :]], axis=1).astype(BF16)
        ucol = group_rows(ut_ref, gl)
        y = _mm(m_ref[gl], ucol) + _mm_nt(q_ref[gl], ht)
        y = y + d_ref[S5_ROWS * gl:S5_ROWS * (gl + 1), :] * ucol.astype(F32)
        gt_ref[:, S5_CH * gl:S5_CH * (gl + 1), :] = _gelu(y).astype(BF16).reshape(S5_T, S5_CH, cols)


def _ssm(ut, utc, p_op, m_op, q_op, a_op, d_rows, n_lat, n_ctx):
    gps = GROUPS_PER_STEP
    cols = ut.shape[2]
    ch = gps * S5_CH
    op = pl.BlockSpec((gps, S5_ROWS, S5_ROWS), lambda i: (i, 0, 0))
    return pl.pallas_call(
        functools.partial(_ssm_kernel, n_lat, n_ctx),
        grid=(S5_GROUPS // gps,),
        in_specs=[pl.BlockSpec((S5_T, ch, cols), lambda i: (0, i, 0)),
                  pl.BlockSpec((S5_T, ch, CTX_LANES), lambda i: (0, i, 0)),
                  op, op, op,
                  pl.BlockSpec((gps, 2 * S5_LANES), lambda i: (i, 0)),
                  pl.BlockSpec((gps * S5_ROWS, 1), lambda i: (i, 0))],
        out_specs=pl.BlockSpec((S5_T, ch, cols), lambda i: (0, i, 0)),
        out_shape=jax.ShapeDtypeStruct(ut.shape, BF16),
        scratch_shapes=[pltpu.VMEM((2, cols * gps, S5_LANES), F32),
                        pltpu.VMEM((2, CTX_LANES * gps, S5_LANES), F32),
                        pltpu.VMEM((2, cols * gps, S5_LANES), F32)],
        compiler_params=_params("arbitrary"),
        name="ssm",
    )(ut, utc, p_op, m_op, q_op, a_op, d_rows)


def _out_kernel(gt_ref, zt_ref, ya_ref, x_ref, gate_ref, wglu_ref, bglu_ref, wout_ref,
                lng_ref, lnb_ref, o_ref):
    g = gt_ref[0]
    a = _mm(wglu_ref[...], g) + bglu_ref[...]
    z = zt_ref[0].astype(F32)
    yb = g.astype(F32) * _sigmoid(a) * (z * _sigmoid(z))
    ybt = yb.T.astype(BF16)
    o = _mm(ya_ref[0], wout_ref[:W_A, :]) + _mm(ybt, wout_ref[W_A:, :])
    r = ALPHA * x_ref[0] + gate_ref[0] * o
    o_ref[0] = _layer_norm(r) * lng_ref[...] + lnb_ref[...]


def _out(gt, zt, ya, x, gate, w_glu_t, b_glu_col, w_out, ln_g, ln_b):
    bsz, seq, _ = x.shape
    n = seq // S5_T
    xv = x.reshape(bsz, n, S5_T * D_MODEL)
    yav = ya.reshape(bsz, n, S5_T * W_A)
    const2 = lambda b, j: (0, 0)
    slab = pl.BlockSpec((1, W_B, n), lambda b, j: (j, 0, b))
    out = pl.pallas_call(
        _out_kernel,
        grid=(bsz, S5_T),
        in_specs=[slab, slab,
                  pl.BlockSpec((1, n, W_A), lambda b, j: (b, 0, j)),
                  pl.BlockSpec((1, n, D_MODEL), lambda b, j: (b, 0, j)),
                  pl.BlockSpec((1, 1, D_MODEL), lambda b, j: (b, 0, 0)),
                  pl.BlockSpec((W_B, W_B), const2, pipeline_mode=pl.Buffered(1)),
                  pl.BlockSpec((W_B, 1), const2),
                  pl.BlockSpec((W_A + W_B, D_MODEL), const2, pipeline_mode=pl.Buffered(1)),
                  pl.BlockSpec((1, D_MODEL), const2),
                  pl.BlockSpec((1, D_MODEL), const2)],
        out_specs=pl.BlockSpec((1, n, D_MODEL), lambda b, j: (b, 0, j)),
        out_shape=jax.ShapeDtypeStruct(xv.shape, F32),
        compiler_params=_params("arbitrary", "arbitrary"),
        name="out",
    )(gt, zt, yav, xv, gate, w_glu_t, b_glu_col, w_out, ln_g, ln_b)
    return out.reshape(bsz, seq, D_MODEL)


def kernel(x, c, ctx, c_ctx, w_ada, b_ada, w_in, sgu_ln_g, sgu_ln_b, w_spatial, b_spatial,
           s5_lam_re, s5_lam_im, s5_log_step, s5_b_re, s5_b_im, s5_c_re, s5_c_im,
           s5_d, w_glu, b_glu, w_out, ln_g, ln_b):
    depth = w_ada.shape[0]
    assert depth == 1, "context update between layers is not implemented"
    bsz, seq, d = x.shape
    assert d == D_MODEL and bsz + 1 <= 8
    assert seq % (S5_T * 8) == 0 and ctx.shape[1] % S5_T == 0
    assert bsz * (ctx.shape[1] // S5_T) <= CTX_LANES
    i = 0
    col_b0 = 3 * W_A

    c_rows = jnp.concatenate([c, c_ctx[None], jnp.zeros((8 - bsz - 1, d), F32)], axis=0)
    mods = _mods(c_rows, w_ada[i], b_ada[i][None])
    shift_x = mods[:bsz, None, :d]
    scale_x = mods[:bsz, None, d:2 * d]
    gate_x = mods[:bsz, None, 2 * d:]
    shift_c = mods[bsz:bsz + 1, :d]
    scale_c = mods[bsz:bsz + 1, d:2 * d]

    w_a = w_in[i][:, :col_b0].astype(BF16)
    w_bt = w_in[i][:, col_b0:].T.astype(BF16)

    p_op, m_op, q_op, a_op = _s5_ops(s5_lam_re[i], s5_lam_im[i], s5_log_step[i],
                                     s5_b_re[i], s5_b_im[i], s5_c_re[i], s5_c_im[i])
    d_rows = jnp.broadcast_to(s5_d[i].reshape(S5_GROUPS, 1, S5_CH),
                              (S5_GROUPS, S5_T, S5_CH)).reshape(S5_GROUPS * S5_ROWS, 1)

    b_sp = jnp.repeat(b_spatial[i].T, HEAD_DIM_A, axis=1)
    ya = _branch_a(x, shift_x, scale_x, w_a, sgu_ln_g[i][None], sgu_ln_b[i][None],
                   w_spatial[i].astype(BF16), b_sp)

    ut, zt = _proj_b(x, shift_x, scale_x, w_bt)
    utc = _proj_ctx(ctx, shift_c, scale_c, w_bt[:W_B])
    gt = _ssm(ut, utc, p_op, m_op, q_op, a_op.reshape(S5_GROUPS, 2 * S5_LANES), d_rows,
              seq // S5_T, ctx.shape[1] // S5_T)

    return _out(gt, zt, ya, x, gate_x, w_glu[i].T.astype(BF16), b_glu[i][:, None],
                w_out[i].astype(BF16), ln_g[i][None], ln_b[i][None])
```

```python
import functools
import math

import jax
import jax.numpy as jnp
from jax import lax
from jax.experimental import pallas as pl
from jax.experimental.pallas import tpu as pltpu

F32 = jnp.float32
BF16 = jnp.bfloat16

D_MODEL = 2048
W_A = 1024
W_B = 1024
N_HEADS_A = 8
HEAD_DIM_A = 128
GMLP_CHUNK = 128
S5_GROUPS = 64
S5_CH = 16
S5_STATE = 64
S5_T = 16
S5_ROWS = S5_T * S5_CH
S5_LANES = 2 * S5_STATE
GROUPS_PER_STEP = 8
LN_EPS = 1e-6
ALPHA = 2.0 ** 0.25
SQRT_HALF = math.sqrt(0.5)
VMEM_LIMIT = 56 * 1024 * 1024


def _sigmoid(x):
    return 1.0 / (1.0 + jnp.exp(-x))


def _gelu(x):
    return 0.5 * x * (1.0 + lax.erf(x * SQRT_HALF))


def _layer_norm(x):
    mu = jnp.mean(x, axis=-1, keepdims=True)
    xc = x - mu
    var = jnp.mean(xc * xc, axis=-1, keepdims=True)
    return xc * lax.rsqrt(var + LN_EPS)


def _mm(a, b):
    return jnp.dot(a, b, preferred_element_type=F32)


def _mm_nt(a, b):
    return lax.dot_general(a, b, (((1,), (1,)), ((), ())), preferred_element_type=F32)


def _params(*sem):
    return pltpu.CompilerParams(dimension_semantics=sem, vmem_limit_bytes=VMEM_LIMIT)


def _mods_kernel(c_ref, w_ref, b_ref, o_ref):
    c = c_ref[...]
    a = c * _sigmoid(c)
    o_ref[...] = jnp.dot(a, w_ref[...], precision=lax.Precision.HIGHEST,
                         preferred_element_type=F32) + b_ref[...]


def _mods(c_rows, w_ada, b_ada):
    n_out = w_ada.shape[1]
    tn = 512
    return pl.pallas_call(
        _mods_kernel,
        grid=(n_out // tn,),
        in_specs=[pl.BlockSpec((8, D_MODEL), lambda i: (0, 0)),
                  pl.BlockSpec((D_MODEL, tn), lambda i: (0, i)),
                  pl.BlockSpec((1, tn), lambda i: (0, i))],
        out_specs=pl.BlockSpec((8, tn), lambda i: (0, i)),
        out_shape=jax.ShapeDtypeStruct((8, n_out), F32),
        compiler_params=_params("arbitrary"),
        name="mods",
    )(c_rows, w_ada, b_ada)


def _s5_ops_kernel(lre_ref, lim_ref, lstep_ref, cre_ref, cim_ref, bre_ref, bim_ref,
                   p_ref, m_ref, q_ref, a_ref):
    lr = lre_ref[0]
    li = lim_ref[0]
    step = jnp.exp(lstep_ref[0])
    dr = lr * step
    di = li * step
    mag = jnp.exp(dr)
    ab_re = mag * jnp.cos(di)
    ab_im = mag * jnp.sin(di)
    den = lr * lr + li * li
    nr = ab_re - 1.0
    ni = ab_im
    f_re = (nr * lr + ni * li) / den
    f_im = (ni * lr - nr * li) / den
    b_re = bre_ref[0]
    b_im = bim_ref[0]
    bb_re = f_re * b_re - f_im * b_im
    bb_im = f_re * b_im + f_im * b_re
    c_re = cre_ref[0]
    c_im = cim_ref[0]

    lane = lax.broadcasted_iota(jnp.int32, (S5_T, S5_LANES), 1)
    pos = lax.broadcasted_iota(jnp.int32, (S5_T, S5_LANES), 0)
    is_fwd = lane < S5_STATE

    def power(e):
        e = e.astype(F32)
        r = jnp.exp(e * dr)
        return r * jnp.cos(e * di), r * jnp.sin(e * di)

    pp_re, pp_im = power(jnp.where(is_fwd, S5_T - 1 - pos, pos))
    pt_re = (pp_re[:, None, :] * bb_re[None] - pp_im[:, None, :] * bb_im[None]).reshape(S5_ROWS, S5_LANES)
    pt_im = (pp_re[:, None, :] * bb_im[None] + pp_im[:, None, :] * bb_re[None]).reshape(S5_ROWS, S5_LANES)
    pt = jnp.concatenate([pt_re, pt_im], axis=1)
    p_ref[0] = pt.T.astype(BF16)

    qq_re, qq_im = power(jnp.where(is_fwd, pos + 1, S5_T - pos))
    q_re = (qq_re[:, None, :] * c_re[None] - qq_im[:, None, :] * c_im[None]).reshape(S5_ROWS, S5_LANES)
    q_im = (qq_re[:, None, :] * c_im[None] + qq_im[:, None, :] * c_re[None]).reshape(S5_ROWS, S5_LANES)
    q_ref[0, :, :S5_LANES] = q_re.astype(BF16)
    q_ref[0, :, S5_LANES:] = (-q_im).astype(BF16)

    e16 = jnp.full((1, S5_LANES), float(S5_T), F32)
    r16 = jnp.exp(e16 * dr)
    a_ref[0, :, :S5_LANES] = r16 * jnp.cos(e16 * di)
    a_ref[0, :, S5_LANES:] = r16 * jnp.sin(e16 * di)

    is_fwd_c = is_fwd[:S5_CH]
    zero = jnp.zeros_like(c_re)
    sel = jnp.concatenate([
        jnp.concatenate([jnp.where(is_fwd_c, c_re, zero), jnp.where(is_fwd_c, -c_im, zero)], axis=1),
        jnp.concatenate([jnp.where(is_fwd_c, zero, c_re), jnp.where(is_fwd_c, zero, -c_im)], axis=1),
    ], axis=0)
    w = lax.dot_general(sel, pt, (((1,), (1,)), ((), ())), precision=lax.Precision.HIGHEST,
                        preferred_element_type=F32)
    w_f = w[:S5_CH]
    w_b = w[S5_CH:]
    col = lax.broadcasted_iota(jnp.int32, (S5_CH, S5_ROWS), 1)
    for i in range(S5_T):
        lo = pltpu.roll(w_f, (S5_CH * (i + 1)) % S5_ROWS, axis=1)
        hi = pltpu.roll(w_b, S5_CH * i, axis=1) if i else w_b
        blk = jnp.where(col < S5_CH * (i + 1), lo, 0.0) + jnp.where(col >= S5_CH * i, hi, 0.0)
        m_ref[0, S5_CH * i:S5_CH * (i + 1), :] = blk.astype(BF16)


def _s5_ops(lam_re, lam_im, log_step, b_re, b_im, c_re, c_im):
    g = S5_GROUPS

    def lanes_dp(v):
        return jnp.transpose(v, (1, 0, 2)).reshape(g, 1, S5_LANES)

    lre = lanes_dp(lam_re)
    lim = lanes_dp(lam_im)
    lstep = lanes_dp(jnp.broadcast_to(log_step[:, :, None], (2, g, S5_STATE)))
    cre = jnp.transpose(c_re, (1, 2, 0, 3)).reshape(g, S5_CH, S5_LANES)
    cim = jnp.transpose(c_im, (1, 2, 0, 3)).reshape(g, S5_CH, S5_LANES)
    bre = jnp.transpose(b_re, (1, 3, 0, 2)).reshape(g, S5_CH, S5_LANES)
    bim = jnp.transpose(b_im, (1, 3, 0, 2)).reshape(g, S5_CH, S5_LANES)
    vec = pl.BlockSpec((1, 1, S5_LANES), lambda i: (i, 0, 0))
    mat = pl.BlockSpec((1, S5_CH, S5_LANES), lambda i: (i, 0, 0))
    op = pl.BlockSpec((1, S5_ROWS, S5_ROWS), lambda i: (i, 0, 0))
    op_shape = jax.ShapeDtypeStruct((g, S5_ROWS, S5_ROWS), BF16)
    return pl.pallas_call(
        _s5_ops_kernel,
        grid=(g,),
        in_specs=[vec, vec, vec, mat, mat, mat, mat],
        out_specs=[op, op, op, pl.BlockSpec((1, 1, 2 * S5_LANES), lambda i: (i, 0, 0))],
        out_shape=[op_shape, op_shape, op_shape,
                   jax.ShapeDtypeStruct((g, 1, 2 * S5_LANES), F32)],
        compiler_params=_params("arbitrary"),
        name="s5_ops",
    )(lre, lim, lstep, cre, cim, bre, bim)


def _branch_a_kernel(x_ref, shift_ref, scale_ref, w_ref, lng_ref, lnb_ref, ws_ref, bsp_ref,
                     ya_ref, zb_ref):
    x = x_ref[0]
    h = _layer_norm(x) * (1.0 + scale_ref[0]) + shift_ref[0]
    p = _mm(h.astype(BF16), w_ref[...])
    u = _gelu(p[:, :W_A])
    v = _gelu(p[:, W_A:2 * W_A])
    z = p[:, 2 * W_A:3 * W_A]
    zb = p[:, 3 * W_A:]
    zb_ref[0] = (zb * _sigmoid(zb)).astype(BF16)
    gate = u * (z * _sigmoid(z))
    vn = (_layer_norm(v) * lng_ref[...] + lnb_ref[...]).astype(BF16)
    tm = x.shape[0]
    for ck in range(tm // GMLP_CHUNK):
        rows = slice(ck * GMLP_CHUNK, (ck + 1) * GMLP_CHUNK)
        for hd in range(N_HEADS_A):
            cols = slice(hd * HEAD_DIM_A, (hd + 1) * HEAD_DIM_A)
            mixed = _mm(ws_ref[hd], vn[rows, cols]) + bsp_ref[:, cols]
            ya_ref[0, rows, cols] = (gate[rows, cols] * mixed).astype(BF16)


def _branch_a(x, shift, scale, w_a, sgu_g, sgu_b, w_s, b_sp, tm=512):
    bsz, seq, _ = x.shape
    const2 = lambda b, i: (0, 0)
    return pl.pallas_call(
        _branch_a_kernel,
        grid=(bsz, seq // tm),
        in_specs=[pl.BlockSpec((1, tm, D_MODEL), lambda b, i: (b, i, 0)),
                  pl.BlockSpec((1, 1, D_MODEL), lambda b, i: (b, 0, 0)),
                  pl.BlockSpec((1, 1, D_MODEL), lambda b, i: (b, 0, 0)),
                  pl.BlockSpec((D_MODEL, 4 * W_A), const2, pipeline_mode=pl.Buffered(1)),
                  pl.BlockSpec((1, W_A), const2),
                  pl.BlockSpec((1, W_A), const2),
                  pl.BlockSpec((N_HEADS_A, GMLP_CHUNK, GMLP_CHUNK), lambda b, i: (0, 0, 0)),
                  pl.BlockSpec((GMLP_CHUNK, W_A), const2)],
        out_specs=[pl.BlockSpec((1, tm, W_A), lambda b, i: (b, i, 0)),
                   pl.BlockSpec((1, tm, W_B), lambda b, i: (b, i, 0))],
        out_shape=[jax.ShapeDtypeStruct((bsz, seq, W_A), BF16),
                   jax.ShapeDtypeStruct((bsz, seq, W_B), BF16)],
        compiler_params=_params("arbitrary", "arbitrary"),
        name="branch_a",
    )(x, shift, scale, w_a, sgu_g, sgu_b, w_s, b_sp)


SUBLANES = 8
LANES = 128
CHUNKS_PER_STEP = 128


def _chunk_major_rows(h, h_sc, n_chunks):
    n_slabs = h.shape[1] // LANES
    for s in range(n_slabs):
        h_sc[s, :h.shape[0], :] = h[:, LANES * s:LANES * (s + 1)]
    return [jnp.concatenate([h_sc[s, pl.ds(jl, n_chunks, stride=SUBLANES), :] for s in range(n_slabs)],
                            axis=1) for jl in range(SUBLANES)]


def _proj_b_kernel(x_ref, shift_ref, scale_ref, wt_ref, ut_ref, h_sc):
    nc = x_ref.shape[0]
    x = x_ref[...].reshape(nc * SUBLANES, D_MODEL)
    h = _layer_norm(x) * (1.0 + scale_ref[0]) + shift_ref[0]
    hp = jnp.concatenate(_chunk_major_rows(h, h_sc, nc), axis=0).astype(BF16)
    r = _mm_nt(wt_ref[...], hp)
    for jl in range(SUBLANES):
        ut_ref[jl] = r[:, nc * jl:nc * (jl + 1)].astype(BF16)


def _proj_b(x, shift, scale, w_ut):
    bsz, seq, _ = x.shape
    n = seq // S5_T
    halves = S5_T // SUBLANES
    nc = CHUNKS_PER_STEP
    x5 = x.reshape(bsz, n, halves, SUBLANES, D_MODEL)
    return pl.pallas_call(
        _proj_b_kernel,
        grid=(bsz, n // nc, halves),
        in_specs=[pl.BlockSpec((None, nc, None, SUBLANES, D_MODEL), lambda b, t, h: (b, t, h, 0, 0)),
                  pl.BlockSpec((1, 1, D_MODEL), lambda b, t, h: (b, 0, 0)),
                  pl.BlockSpec((1, 1, D_MODEL), lambda b, t, h: (b, 0, 0)),
                  pl.BlockSpec((W_B, D_MODEL), lambda b, t, h: (0, 0), pipeline_mode=pl.Buffered(1))],
        out_specs=pl.BlockSpec((SUBLANES, W_B, nc), lambda b, t, h: (h, 0, b * (n // nc) + t)),
        out_shape=jax.ShapeDtypeStruct((S5_T, W_B, bsz * n), BF16),
        scratch_shapes=[pltpu.VMEM((D_MODEL // LANES, nc * SUBLANES, LANES), F32)],
        compiler_params=_params("arbitrary", "arbitrary", "arbitrary"),
        name="proj_b",
    )(x5, shift, scale, w_ut)


CTX_LANES = 128


def _proj_ctx_kernel(x_ref, shift_ref, scale_ref, wt_ref, ut_ref, h_sc):
    bsz, nc = x_ref.shape[0], x_ref.shape[1]
    x = x_ref[...].reshape(bsz * nc * SUBLANES, D_MODEL)
    h = _layer_norm(x) * (1.0 + scale_ref[...]) + shift_ref[...]
    hp = jnp.concatenate(_chunk_major_rows(h, h_sc, bsz * nc), axis=0).astype(BF16)
    r = _mm_nt(wt_ref[...], hp)
    per = bsz * nc
    for jl in range(SUBLANES):
        win = (per * jl) // LANES
        off = (per * jl) % LANES
        blk = r[:, LANES * win:LANES * (win + 1)]
        if off:
            blk = pltpu.roll(blk, LANES - off, axis=1)
        ut_ref[jl] = blk.astype(BF16)


def _proj_ctx(ctx, shift_c, scale_c, w_ut):
    bsz, lc, _ = ctx.shape
    nc = lc // S5_T
    halves = S5_T // SUBLANES
    c5 = ctx.reshape(bsz, nc, halves, SUBLANES, D_MODEL)
    rows = bsz * nc * SUBLANES
    assert rows % LANES == 0 and LANES % (bsz * nc) == 0
    return pl.pallas_call(
        _proj_ctx_kernel,
        grid=(halves,),
        in_specs=[pl.BlockSpec((bsz, nc, None, SUBLANES, D_MODEL), lambda h: (0, 0, h, 0, 0)),
                  pl.BlockSpec((1, D_MODEL), lambda h: (0, 0)),
                  pl.BlockSpec((1, D_MODEL), lambda h: (0, 0)),
                  pl.BlockSpec((W_B, D_MODEL), lambda h: (0, 0))],
        out_specs=pl.BlockSpec((SUBLANES, W_B, CTX_LANES), lambda h: (h, 0, 0)),
        out_shape=jax.ShapeDtypeStruct((S5_T, W_B, CTX_LANES), BF16),
        scratch_shapes=[pltpu.VMEM((D_MODEL // LANES, rows, LANES), F32)],
        compiler_params=_params("arbitrary"),
        name="proj_ctx",
    )(c5, shift_c, scale_c, w_ut)


def _ssm_kernel(n_lat, n_ctx, ut_ref, utc_ref, p_ref, m_ref, q_ref, a_ref, d_ref, gt_ref,
                s_sc, sc_sc, h_sc):
    gps = GROUPS_PER_STEP
    bsz = ut_ref.shape[2] // n_lat
    cols = bsz * n_lat

    def group_rows(ref, gl):
        blk = ref[:, S5_CH * gl:S5_CH * (gl + 1), :]
        return blk.reshape(S5_ROWS, blk.shape[2])

    for gl in range(gps):
        st = _mm(p_ref[gl], group_rows(ut_ref, gl)).T
        s_sc[0, pl.ds(gl, cols, stride=gps), :] = st[:, :S5_LANES]
        s_sc[1, pl.ds(gl, cols, stride=gps), :] = st[:, S5_LANES:]
        sc = _mm(p_ref[gl], group_rows(utc_ref, gl)).T
        sc_sc[0, pl.ds(gl, CTX_LANES, stride=gps), :] = sc[:, :S5_LANES]
        sc_sc[1, pl.ds(gl, CTX_LANES, stride=gps), :] = sc[:, S5_LANES:]

    a_re = a_ref[:, :S5_LANES]
    a_im = a_ref[:, S5_LANES:]
    is_fwd = lax.broadcasted_iota(jnp.int32, (gps, S5_LANES), 1) < S5_STATE
    is_bwd = jnp.logical_not(is_fwd)

    def advance(h_re, h_im, src, row_f, row_b):
        s_re = jnp.where(is_fwd, src[0, pl.ds(row_f, gps), :], src[0, pl.ds(row_b, gps), :])
        s_im = jnp.where(is_fwd, src[1, pl.ds(row_f, gps), :], src[1, pl.ds(row_b, gps), :])
        return (a_re * h_re - a_im * h_im + s_re, a_re * h_im + a_im * h_re + s_im)

    def ctx_step(t, carry):
        out = []
        for b in range(bsz):
            row_f = pl.multiple_of((b * n_ctx + t) * gps, gps)
            row_b = pl.multiple_of((b * n_ctx + n_ctx - 1 - t) * gps, gps)
            out.extend(advance(carry[2 * b], carry[2 * b + 1], sc_sc, row_f, row_b))
        return tuple(out)

    def lat_step(t, carry):
        out = []
        for b in range(bsz):
            h_re, h_im = carry[2 * b], carry[2 * b + 1]
            row_f = pl.multiple_of((b * n_lat + t) * gps, gps)
            row_b = pl.multiple_of((b * n_lat + n_lat - 1 - t) * gps, gps)
            pltpu.store(h_sc.at[0, pl.ds(row_f, gps), :], h_re, mask=is_fwd)
            pltpu.store(h_sc.at[1, pl.ds(row_f, gps), :], h_im, mask=is_fwd)
            pltpu.store(h_sc.at[0, pl.ds(row_b, gps), :], h_re, mask=is_bwd)
            pltpu.store(h_sc.at[1, pl.ds(row_b, gps), :], h_im, mask=is_bwd)
            out.extend(advance(h_re, h_im, s_sc, row_f, row_b))
        return tuple(out)

    zero = jnp.zeros((gps, S5_LANES), F32)
    carry = lax.fori_loop(0, n_ctx, ctx_step, (zero,) * (2 * bsz))
    lax.fori_loop(0, n_lat, lat_step, carry)

    for gl in range(gps):
        ht = jnp.concatenate([h_sc[0, pl.ds(gl, cols, stride=gps), :],
                              h_sc[1, pl.ds(gl, cols, stride=gps), :]], axis=1).astype(BF16)
        ucol = group_rows(ut_ref, gl)
        y = _mm(m_ref[gl], ucol) + _mm_nt(q_ref[gl], ht)
        y = y + d_ref[S5_ROWS * gl:S5_ROWS * (gl + 1), :] * ucol.astype(F32)
        gt_ref[:, S5_CH * gl:S5_CH * (gl + 1), :] = _gelu(y).astype(BF16).reshape(S5_T, S5_CH, cols)


def _ssm(ut, utc, p_op, m_op, q_op, a_op, d_rows, n_lat, n_ctx):
    gps = GROUPS_PER_STEP
    cols = ut.shape[2]
    ch = gps * S5_CH
    op = pl.BlockSpec((gps, S5_ROWS, S5_ROWS), lambda i: (i, 0, 0))
    return pl.pallas_call(
        functools.partial(_ssm_kernel, n_lat, n_ctx),
        grid=(S5_GROUPS // gps,),
        in_specs=[pl.BlockSpec((S5_T, ch, cols), lambda i: (0, i, 0)),
                  pl.BlockSpec((S5_T, ch, CTX_LANES), lambda i: (0, i, 0)),
                  op, op, op,
                  pl.BlockSpec((gps, 2 * S5_LANES), lambda i: (i, 0)),
                  pl.BlockSpec((gps * S5_ROWS, 1), lambda i: (i, 0))],
        out_specs=pl.BlockSpec((S5_T, ch, cols), lambda i: (0, i, 0)),
        out_shape=jax.ShapeDtypeStruct(ut.shape, BF16),
        scratch_shapes=[pltpu.VMEM((2, cols * gps, S5_LANES), F32),
                        pltpu.VMEM((2, CTX_LANES * gps, S5_LANES), F32),
                        pltpu.VMEM((2, cols * gps, S5_LANES), F32)],
        compiler_params=_params("arbitrary"),
        name="ssm",
    )(ut, utc, p_op, m_op, q_op, a_op, d_rows)


def _glu_kernel(gt_ref, wglu_ref, bglu_ref, yb_ref, t_sc):
    nc = gt_ref.shape[2]
    n_slabs = W_B // LANES
    for half in range(S5_T // SUBLANES):
        g = jnp.concatenate([gt_ref[SUBLANES * half + jl] for jl in range(SUBLANES)], axis=1)
        a = _mm(wglu_ref[...], g) + bglu_ref[...]
        ybt = (g.astype(F32) * _sigmoid(a)).T
        for jl in range(SUBLANES):
            for s in range(n_slabs):
                t_sc[s, pl.ds(SUBLANES * half + jl, nc, stride=S5_T), :] = (
                    ybt[nc * jl:nc * (jl + 1), LANES * s:LANES * (s + 1)])
    yb_ref[0] = jnp.concatenate([t_sc[s] for s in range(n_slabs)], axis=1).astype(BF16)


def _glu(gt, w_glu_t, b_glu_col, bsz):
    nc = CHUNKS_PER_STEP
    tiles = gt.shape[2] // nc
    per_batch = tiles // bsz
    rows = nc * S5_T
    return pl.pallas_call(
        _glu_kernel,
        grid=(tiles,),
        in_specs=[pl.BlockSpec((S5_T, W_B, nc), lambda i: (0, 0, i)),
                  pl.BlockSpec((W_B, W_B), lambda i: (0, 0), pipeline_mode=pl.Buffered(1)),
                  pl.BlockSpec((W_B, 1), lambda i: (0, 0))],
        out_specs=pl.BlockSpec((1, rows, W_B), lambda i: (i // per_batch, i % per_batch, 0)),
        out_shape=jax.ShapeDtypeStruct((bsz, per_batch * rows, W_B), BF16),
        scratch_shapes=[pltpu.VMEM((W_B // LANES, rows, LANES), F32)],
        compiler_params=_params("arbitrary"),
        name="glu",
    )(gt, w_glu_t, b_glu_col)


def _out_kernel(x_ref, ya_ref, yb_ref, zb_ref, gate_ref, wout_ref, lng_ref, lnb_ref, o_ref):
    yb = (yb_ref[0].astype(F32) * zb_ref[0].astype(F32)).astype(BF16)
    o = _mm(ya_ref[0], wout_ref[:W_A, :]) + _mm(yb, wout_ref[W_A:, :])
    r = ALPHA * x_ref[0] + gate_ref[0] * o
    o_ref[0] = _layer_norm(r) * lng_ref[...] + lnb_ref[...]


def _out(x, ya, yb, zb, gate, w_out, ln_g, ln_b, tm=512):
    bsz, seq, _ = x.shape
    const2 = lambda b, i: (0, 0)
    tok = lambda width: pl.BlockSpec((1, tm, width), lambda b, i: (b, i, 0))
    return pl.pallas_call(
        _out_kernel,
        grid=(bsz, seq // tm),
        in_specs=[tok(D_MODEL), tok(W_A), tok(W_B), tok(W_B),
                  pl.BlockSpec((1, 1, D_MODEL), lambda b, i: (b, 0, 0)),
                  pl.BlockSpec((W_A + W_B, D_MODEL), const2, pipeline_mode=pl.Buffered(1)),
                  pl.BlockSpec((1, D_MODEL), const2),
                  pl.BlockSpec((1, D_MODEL), const2)],
        out_specs=tok(D_MODEL),
        out_shape=jax.ShapeDtypeStruct(x.shape, F32),
        compiler_params=_params("arbitrary", "arbitrary"),
        name="out",
    )(x, ya, yb, zb, gate, w_out, ln_g, ln_b)


def kernel(x, c, ctx, c_ctx, w_ada, b_ada, w_in, sgu_ln_g, sgu_ln_b, w_spatial, b_spatial,
           s5_lam_re, s5_lam_im, s5_log_step, s5_b_re, s5_b_im, s5_c_re, s5_c_im,
           s5_d, w_glu, b_glu, w_out, ln_g, ln_b):
    depth = w_ada.shape[0]
    assert depth == 1, "context update between layers is not implemented"
    bsz, seq, d = x.shape
    n_lat = seq // S5_T
    n_ctx = ctx.shape[1] // S5_T
    assert d == D_MODEL and bsz + 1 <= 8
    assert seq % (S5_T * CHUNKS_PER_STEP) == 0 and ctx.shape[1] % S5_T == 0
    assert bsz * n_ctx <= CTX_LANES
    i = 0
    col_b0, col_b1 = 3 * W_A, 3 * W_A + W_B

    c_rows = jnp.concatenate([c, c_ctx[None], jnp.zeros((8 - bsz - 1, d), F32)], axis=0)
    mods = _mods(c_rows, w_ada[i], b_ada[i][None])
    shift_x = mods[:bsz, None, :d]
    scale_x = mods[:bsz, None, d:2 * d]
    gate_x = mods[:bsz, None, 2 * d:]
    shift_c = mods[bsz:bsz + 1, :d]
    scale_c = mods[bsz:bsz + 1, d:2 * d]

    w_a = jnp.concatenate([w_in[i][:, :col_b0], w_in[i][:, col_b1:]], axis=1).astype(BF16)
    w_ut = w_in[i][:, col_b0:col_b1].T.astype(BF16)

    p_op, m_op, q_op, a_op = _s5_ops(s5_lam_re[i], s5_lam_im[i], s5_log_step[i],
                                     s5_b_re[i], s5_b_im[i], s5_c_re[i], s5_c_im[i])
    d_rows = jnp.broadcast_to(s5_d[i].reshape(S5_GROUPS, 1, S5_CH),
                              (S5_GROUPS, S5_T, S5_CH)).reshape(S5_GROUPS * S5_ROWS, 1)

    b_sp = jnp.repeat(b_spatial[i].T, HEAD_DIM_A, axis=1)
    ya, zb = _branch_a(x, shift_x, scale_x, w_a, sgu_ln_g[i][None], sgu_ln_b[i][None],
                       w_spatial[i].astype(BF16), b_sp)

    ut = _proj_b(x, shift_x, scale_x, w_ut)
    utc = _proj_ctx(ctx, shift_c, scale_c, w_ut)
    gt = _ssm(ut, utc, p_op, m_op, q_op, a_op.reshape(S5_GROUPS, 2 * S5_LANES), d_rows, n_lat, n_ctx)
    yb = _glu(gt, w_glu[i].T.astype(BF16), b_glu[i][:, None], bsz)

    return _out(x, ya, yb, zb, gate_x, w_out[i].astype(BF16), ln_g[i][None], ln_b[i][None])
```

```python
import functools
import math

import jax
import jax.numpy as jnp
from jax import lax
from jax.experimental import pallas as pl
from jax.experimental.pallas import tpu as pltpu

F32 = jnp.float32
BF16 = jnp.bfloat16

D_MODEL = 2048
W_A = 1024
W_B = 1024
N_HEADS_A = 8
HEAD_DIM_A = 128
GMLP_CHUNK = 128
S5_GROUPS = 64
S5_CH = 16
S5_STATE = 64
S5_T = 16
S5_ROWS = S5_T * S5_CH
S5_LANES = 2 * S5_STATE
GROUPS_PER_STEP = 8
RECURRENCE_UNROLL = 4
LN_EPS = 1e-6
ALPHA = 2.0 ** 0.25
SQRT_HALF = math.sqrt(0.5)
VMEM_LIMIT = 56 * 1024 * 1024


def _sigmoid(x):
    return 1.0 / (1.0 + jnp.exp(-x))


def _gelu(x):
    return 0.5 * x * (1.0 + lax.erf(x * SQRT_HALF))


def _layer_norm(x):
    mu = jnp.mean(x, axis=-1, keepdims=True)
    xc = x - mu
    var = jnp.mean(xc * xc, axis=-1, keepdims=True)
    return xc * lax.rsqrt(var + LN_EPS)


def _mm(a, b):
    return jnp.dot(a, b, preferred_element_type=F32)


def _mm_nt(a, b):
    return lax.dot_general(a, b, (((1,), (1,)), ((), ())), preferred_element_type=F32)


def _params(*sem):
    return pltpu.CompilerParams(dimension_semantics=sem, vmem_limit_bytes=VMEM_LIMIT)


def _mods_kernel(c_ref, w_ref, b_ref, o_ref):
    c = c_ref[...]
    a = c * _sigmoid(c)
    o_ref[...] = jnp.dot(a, w_ref[...], precision=lax.Precision.HIGHEST,
                         preferred_element_type=F32) + b_ref[...]


def _mods(c_rows, w_ada, b_ada):
    n_out = w_ada.shape[1]
    tn = 512
    return pl.pallas_call(
        _mods_kernel,
        grid=(n_out // tn,),
        in_specs=[pl.BlockSpec((8, D_MODEL), lambda i: (0, 0)),
                  pl.BlockSpec((D_MODEL, tn), lambda i: (0, i)),
                  pl.BlockSpec((1, tn), lambda i: (0, i))],
        out_specs=pl.BlockSpec((8, tn), lambda i: (0, i)),
        out_shape=jax.ShapeDtypeStruct((8, n_out), F32),
        compiler_params=_params("arbitrary"),
        name="mods",
    )(c_rows, w_ada, b_ada)


def _s5_ops_kernel(lre_ref, lim_ref, lstep_ref, cre_ref, cim_ref, bre_ref, bim_ref, d_ref,
                   p_ref, m_ref, q_ref, a_ref):
    for g in range(lre_ref.shape[0]):
        _s5_group_ops(g, lre_ref, lim_ref, lstep_ref, cre_ref, cim_ref, bre_ref, bim_ref, d_ref,
                      p_ref, m_ref, q_ref, a_ref)


def _s5_group_ops(g, lre_ref, lim_ref, lstep_ref, cre_ref, cim_ref, bre_ref, bim_ref, d_ref,
                  p_ref, m_ref, q_ref, a_ref):
    lr = lre_ref[g]
    li = lim_ref[g]
    step = jnp.exp(lstep_ref[g])
    dr = lr * step
    di = li * step
    mag = jnp.exp(dr)
    ab_re = mag * jnp.cos(di)
    ab_im = mag * jnp.sin(di)
    den = lr * lr + li * li
    nr = ab_re - 1.0
    ni = ab_im
    f_re = (nr * lr + ni * li) / den
    f_im = (ni * lr - nr * li) / den
    b_re = bre_ref[g]
    b_im = bim_ref[g]
    bb_re = f_re * b_re - f_im * b_im
    bb_im = f_re * b_im + f_im * b_re
    c_re = cre_ref[g]
    c_im = cim_ref[g]

    lane = lax.broadcasted_iota(jnp.int32, (S5_T, S5_LANES), 1)
    pos = lax.broadcasted_iota(jnp.int32, (S5_T, S5_LANES), 0)
    is_fwd = lane < S5_STATE

    def power(e):
        e = e.astype(F32)
        r = jnp.exp(e * dr)
        return r * jnp.cos(e * di), r * jnp.sin(e * di)

    pp_re, pp_im = power(jnp.where(is_fwd, S5_T - 1 - pos, pos))
    pt_re = (pp_re[:, None, :] * bb_re[None] - pp_im[:, None, :] * bb_im[None]).reshape(S5_ROWS, S5_LANES)
    pt_im = (pp_re[:, None, :] * bb_im[None] + pp_im[:, None, :] * bb_re[None]).reshape(S5_ROWS, S5_LANES)
    pt = jnp.concatenate([pt_re, pt_im], axis=1)
    p_ref[g] = pt.T.astype(BF16)

    qq_re, qq_im = power(jnp.where(is_fwd, pos + 1, S5_T - pos))
    q_re = (qq_re[:, None, :] * c_re[None] - qq_im[:, None, :] * c_im[None]).reshape(S5_ROWS, S5_LANES)
    q_im = (qq_re[:, None, :] * c_im[None] + qq_im[:, None, :] * c_re[None]).reshape(S5_ROWS, S5_LANES)
    q_ref[g, :, :S5_LANES] = q_re.astype(BF16)
    q_ref[g, :, S5_LANES:] = (-q_im).astype(BF16)

    e16 = jnp.full((1, S5_LANES), float(S5_T), F32)
    r16 = jnp.exp(e16 * dr)
    a_ref[g, :, :S5_LANES] = r16 * jnp.cos(e16 * di)
    a_ref[g, :, S5_LANES:] = r16 * jnp.sin(e16 * di)

    is_fwd_c = is_fwd[:S5_CH]
    zero = jnp.zeros_like(c_re)
    sel = jnp.concatenate([
        jnp.concatenate([jnp.where(is_fwd_c, c_re, zero), jnp.where(is_fwd_c, -c_im, zero)], axis=1),
        jnp.concatenate([jnp.where(is_fwd_c, zero, c_re), jnp.where(is_fwd_c, zero, -c_im)], axis=1),
    ], axis=0)
    w = lax.dot_general(sel, pt, (((1,), (1,)), ((), ())), precision=lax.Precision.HIGHEST,
                        preferred_element_type=F32)
    w_f = w[:S5_CH]
    w_b = w[S5_CH:]
    col = lax.broadcasted_iota(jnp.int32, (S5_CH, S5_ROWS), 1)
    ch = lax.broadcasted_iota(jnp.int32, (S5_CH, S5_ROWS), 0)
    d_skip = jnp.concatenate([d_ref[g], d_ref[g]], axis=1)
    for i in range(S5_T):
        lo = pltpu.roll(w_f, (S5_CH * (i + 1)) % S5_ROWS, axis=1)
        hi = pltpu.roll(w_b, S5_CH * i, axis=1) if i else w_b
        blk = jnp.where(col < S5_CH * (i + 1), lo, 0.0) + jnp.where(col >= S5_CH * i, hi, 0.0)
        blk = blk + jnp.where(col == S5_CH * i + ch, d_skip, 0.0)
        m_ref[g, S5_CH * i:S5_CH * (i + 1), :] = blk.astype(BF16)


def _s5_ops(lam_re, lam_im, log_step, b_re, b_im, c_re, c_im, d_skip):
    g = S5_GROUPS
    gps = GROUPS_PER_STEP

    def lanes_dp(v):
        return jnp.transpose(v, (1, 0, 2)).reshape(g, 1, S5_LANES)

    lre = lanes_dp(lam_re)
    lim = lanes_dp(lam_im)
    lstep = lanes_dp(jnp.broadcast_to(log_step[:, :, None], (2, g, S5_STATE)))
    cre = jnp.transpose(c_re, (1, 2, 0, 3)).reshape(g, S5_CH, S5_LANES)
    cim = jnp.transpose(c_im, (1, 2, 0, 3)).reshape(g, S5_CH, S5_LANES)
    bre = jnp.transpose(b_re, (1, 3, 0, 2)).reshape(g, S5_CH, S5_LANES)
    bim = jnp.transpose(b_im, (1, 3, 0, 2)).reshape(g, S5_CH, S5_LANES)
    dsk = jnp.broadcast_to(d_skip.reshape(g, S5_CH, 1), (g, S5_CH, S5_LANES))
    vec = pl.BlockSpec((gps, 1, S5_LANES), lambda i: (i, 0, 0))
    mat = pl.BlockSpec((gps, S5_CH, S5_LANES), lambda i: (i, 0, 0))
    op = pl.BlockSpec((gps, S5_ROWS, S5_ROWS), lambda i: (i, 0, 0))
    op_shape = jax.ShapeDtypeStruct((g, S5_ROWS, S5_ROWS), BF16)
    return pl.pallas_call(
        _s5_ops_kernel,
        grid=(g // gps,),
        in_specs=[vec, vec, vec, mat, mat, mat, mat, mat],
        out_specs=[op, op, op, pl.BlockSpec((gps, 1, 2 * S5_LANES), lambda i: (i, 0, 0))],
        out_shape=[op_shape, op_shape, op_shape,
                   jax.ShapeDtypeStruct((g, 1, 2 * S5_LANES), F32)],
        compiler_params=_params("arbitrary"),
        name="s5_ops",
    )(lre, lim, lstep, cre, cim, bre, bim, dsk)


def _branch_a_kernel(x_ref, shift_ref, scale_ref, wa_ref, wzb_ref, lng_ref, lnb_ref, ws_ref, bsp_ref,
                     ya_ref, zb_ref):
    x = x_ref[0]
    h = (_layer_norm(x) * (1.0 + scale_ref[0]) + shift_ref[0]).astype(BF16)
    p = _mm(h, wa_ref[...])
    zb = _mm(h, wzb_ref[...])
    u = _gelu(p[:, :W_A])
    v = _gelu(p[:, W_A:2 * W_A])
    z = p[:, 2 * W_A:]
    zb_ref[0] = (zb * _sigmoid(zb)).astype(BF16)
    gate = u * (z * _sigmoid(z))
    vn = (_layer_norm(v) * lng_ref[...] + lnb_ref[...]).astype(BF16)
    tm = x.shape[0]
    for ck in range(tm // GMLP_CHUNK):
        rows = slice(ck * GMLP_CHUNK, (ck + 1) * GMLP_CHUNK)
        for hd in range(N_HEADS_A):
            cols = slice(hd * HEAD_DIM_A, (hd + 1) * HEAD_DIM_A)
            mixed = _mm(ws_ref[hd], vn[rows, cols]) + bsp_ref[:, cols]
            ya_ref[0, rows, cols] = (gate[rows, cols] * mixed).astype(BF16)


def _branch_a(x, shift, scale, w_in, sgu_g, sgu_b, w_s, b_sp, tm=512):
    bsz, seq, _ = x.shape
    const2 = lambda b, i: (0, 0)
    zb_block = (3 * W_A + W_B) // W_B
    return pl.pallas_call(
        _branch_a_kernel,
        grid=(bsz, seq // tm),
        in_specs=[pl.BlockSpec((1, tm, D_MODEL), lambda b, i: (b, i, 0)),
                  pl.BlockSpec((1, 1, D_MODEL), lambda b, i: (b, 0, 0)),
                  pl.BlockSpec((1, 1, D_MODEL), lambda b, i: (b, 0, 0)),
                  pl.BlockSpec((D_MODEL, 3 * W_A), const2, pipeline_mode=pl.Buffered(1)),
                  pl.BlockSpec((D_MODEL, W_B), lambda b, i: (0, zb_block), pipeline_mode=pl.Buffered(1)),
                  pl.BlockSpec((1, W_A), const2),
                  pl.BlockSpec((1, W_A), const2),
                  pl.BlockSpec((N_HEADS_A, GMLP_CHUNK, GMLP_CHUNK), lambda b, i: (0, 0, 0)),
                  pl.BlockSpec((GMLP_CHUNK, W_A), const2)],
        out_specs=[pl.BlockSpec((1, tm, W_A), lambda b, i: (b, i, 0)),
                   pl.BlockSpec((1, tm, W_B), lambda b, i: (b, i, 0))],
        out_shape=[jax.ShapeDtypeStruct((bsz, seq, W_A), BF16),
                   jax.ShapeDtypeStruct((bsz, seq, W_B), BF16)],
        compiler_params=_params("arbitrary", "arbitrary"),
        name="branch_a",
    )(x, shift, scale, w_in, w_in, sgu_g, sgu_b, w_s, b_sp)


SUBLANES = 8
LANES = 128
CHUNKS_PER_STEP = 128


def _chunk_major_rows(h, h_sc, n_chunks):
    n_slabs = h.shape[1] // LANES
    for s in range(n_slabs):
        h_sc[s, :h.shape[0], :] = h[:, LANES * s:LANES * (s + 1)]
    return [jnp.concatenate([h_sc[s, pl.ds(jl, n_chunks, stride=SUBLANES), :] for s in range(n_slabs)],
                            axis=1) for jl in range(SUBLANES)]


def _proj_b_kernel(x_ref, shift_ref, scale_ref, w_ref, ut_ref, h_sc, wt_sc):
    first = (pl.program_id(0) == 0) & (pl.program_id(1) == 0) & (pl.program_id(2) == 0)

    @pl.when(first)
    def _():
        wt_sc[...] = w_ref[...].T

    nc = x_ref.shape[0]
    x = x_ref[...].reshape(nc * SUBLANES, D_MODEL)
    h = _layer_norm(x) * (1.0 + scale_ref[0]) + shift_ref[0]
    hp = jnp.concatenate(_chunk_major_rows(h, h_sc, nc), axis=0).astype(BF16)
    r = _mm_nt(wt_sc[...], hp)
    for jl in range(SUBLANES):
        ut_ref[jl] = r[:, nc * jl:nc * (jl + 1)].astype(BF16)


def _proj_b(x, shift, scale, w_in):
    ub_block = (3 * W_A) // W_B
    bsz, seq, _ = x.shape
    n = seq // S5_T
    halves = S5_T // SUBLANES
    nc = CHUNKS_PER_STEP
    x5 = x.reshape(bsz, n, halves, SUBLANES, D_MODEL)
    return pl.pallas_call(
        _proj_b_kernel,
        grid=(bsz, n // nc, halves),
        in_specs=[pl.BlockSpec((None, nc, None, SUBLANES, D_MODEL), lambda b, t, h: (b, t, h, 0, 0)),
                  pl.BlockSpec((1, 1, D_MODEL), lambda b, t, h: (b, 0, 0)),
                  pl.BlockSpec((1, 1, D_MODEL), lambda b, t, h: (b, 0, 0)),
                  pl.BlockSpec((D_MODEL, W_B), lambda b, t, h: (0, ub_block), pipeline_mode=pl.Buffered(1))],
        out_specs=pl.BlockSpec((SUBLANES, W_B, nc), lambda b, t, h: (h, 0, b * (n // nc) + t)),
        out_shape=jax.ShapeDtypeStruct((S5_T, W_B, bsz * n), BF16),
        scratch_shapes=[pltpu.VMEM((D_MODEL // LANES, nc * SUBLANES, LANES), F32),
                        pltpu.VMEM((W_B, D_MODEL), BF16)],
        compiler_params=_params("arbitrary", "arbitrary", "arbitrary"),
        name="proj_b",
    )(x5, shift, scale, w_in)


CTX_LANES = 128


def _proj_ctx_kernel(x_ref, shift_ref, scale_ref, w_ref, ut_ref, h_sc, wt_sc):
    @pl.when(pl.program_id(0) == 0)
    def _():
        wt_sc[...] = w_ref[...].T

    bsz, nc = x_ref.shape[0], x_ref.shape[1]
    x = x_ref[...].reshape(bsz * nc * SUBLANES, D_MODEL)
    h = _layer_norm(x) * (1.0 + scale_ref[...]) + shift_ref[...]
    hp = jnp.concatenate(_chunk_major_rows(h, h_sc, bsz * nc), axis=0).astype(BF16)
    r = _mm_nt(wt_sc[...], hp)
    per = bsz * nc
    for jl in range(SUBLANES):
        win = (per * jl) // LANES
        off = (per * jl) % LANES
        blk = r[:, LANES * win:LANES * (win + 1)]
        if off:
            blk = pltpu.roll(blk, LANES - off, axis=1)
        ut_ref[jl] = blk.astype(BF16)


def _proj_ctx(ctx, shift_c, scale_c, w_in):
    ub_block = (3 * W_A) // W_B
    bsz, lc, _ = ctx.shape
    nc = lc // S5_T
    halves = S5_T // SUBLANES
    c5 = ctx.reshape(bsz, nc, halves, SUBLANES, D_MODEL)
    rows = bsz * nc * SUBLANES
    assert rows % LANES == 0 and LANES % (bsz * nc) == 0
    return pl.pallas_call(
        _proj_ctx_kernel,
        grid=(halves,),
        in_specs=[pl.BlockSpec((bsz, nc, None, SUBLANES, D_MODEL), lambda h: (0, 0, h, 0, 0)),
                  pl.BlockSpec((1, D_MODEL), lambda h: (0, 0)),
                  pl.BlockSpec((1, D_MODEL), lambda h: (0, 0)),
                  pl.BlockSpec((D_MODEL, W_B), lambda h: (0, ub_block), pipeline_mode=pl.Buffered(1))],
        out_specs=pl.BlockSpec((SUBLANES, W_B, CTX_LANES), lambda h: (h, 0, 0)),
        out_shape=jax.ShapeDtypeStruct((S5_T, W_B, CTX_LANES), BF16),
        scratch_shapes=[pltpu.VMEM((D_MODEL // LANES, rows, LANES), F32),
                        pltpu.VMEM((W_B, D_MODEL), BF16)],
        compiler_params=_params("arbitrary"),
        name="proj_ctx",
    )(c5, shift_c, scale_c, w_in)


def _ssm_kernel(n_lat, n_ctx, ut_ref, utc_ref, p_ref, m_ref, q_ref, a_ref, gt_ref,
                s_sc, sc_sc, h_sc):
    gps = GROUPS_PER_STEP
    bsz = ut_ref.shape[2] // n_lat
    cols = bsz * n_lat

    def group_rows(ref, gl):
        blk = ref[:, S5_CH * gl:S5_CH * (gl + 1), :]
        return blk.reshape(S5_ROWS, blk.shape[2])

    for gl in range(gps):
        st = _mm(p_ref[gl], group_rows(ut_ref, gl)).T
        s_sc[0, pl.ds(gl, cols, stride=gps), :] = st[:, :S5_LANES]
        s_sc[1, pl.ds(gl, cols, stride=gps), :] = st[:, S5_LANES:]
        sc = _mm(p_ref[gl], group_rows(utc_ref, gl)).T
        sc_sc[0, pl.ds(gl, CTX_LANES, stride=gps), :] = sc[:, :S5_LANES]
        sc_sc[1, pl.ds(gl, CTX_LANES, stride=gps), :] = sc[:, S5_LANES:]

    a_re = a_ref[:, :S5_LANES]
    a_im = a_ref[:, S5_LANES:]
    is_fwd = lax.broadcasted_iota(jnp.int32, (gps, S5_LANES), 1) < S5_STATE
    is_bwd = jnp.logical_not(is_fwd)

    def advance(h_re, h_im, src, row_f, row_b):
        s_re = jnp.where(is_fwd, src[0, pl.ds(row_f, gps), :], src[0, pl.ds(row_b, gps), :])
        s_im = jnp.where(is_fwd, src[1, pl.ds(row_f, gps), :], src[1, pl.ds(row_b, gps), :])
        return (a_re * h_re - a_im * h_im + s_re, a_re * h_im + a_im * h_re + s_im)

    def ctx_step(t, carry):
        out = []
        for b in range(bsz):
            row_f = pl.multiple_of((b * n_ctx + t) * gps, gps)
            row_b = pl.multiple_of((b * n_ctx + n_ctx - 1 - t) * gps, gps)
            out.extend(advance(carry[2 * b], carry[2 * b + 1], sc_sc, row_f, row_b))
        return tuple(out)

    def lat_step(t, carry):
        out = []
        for b in range(bsz):
            h_re, h_im = carry[2 * b], carry[2 * b + 1]
            row_f = pl.multiple_of((b * n_lat + t) * gps, gps)
            row_b = pl.multiple_of((b * n_lat + n_lat - 1 - t) * gps, gps)
            pltpu.store(h_sc.at[0, pl.ds(row_f, gps), :], h_re, mask=is_fwd)
            pltpu.store(h_sc.at[1, pl.ds(row_f, gps), :], h_im, mask=is_fwd)
            pltpu.store(h_sc.at[0, pl.ds(row_b, gps), :], h_re, mask=is_bwd)
            pltpu.store(h_sc.at[1, pl.ds(row_b, gps), :], h_im, mask=is_bwd)
            out.extend(advance(h_re, h_im, s_sc, row_f, row_b))
        return tuple(out)

    zero = jnp.zeros((gps, S5_LANES), F32)
    carry = lax.fori_loop(0, n_ctx, ctx_step, (zero,) * (2 * bsz), unroll=RECURRENCE_UNROLL)
    lax.fori_loop(0, n_lat, lat_step, carry, unroll=RECURRENCE_UNROLL)

    for gl in range(gps):
        ht = jnp.concatenate([h_sc[0, pl.ds(gl, cols, stride=gps), :],
                              h_sc[1, pl.ds(gl, cols, stride=gps), :]], axis=1).astype(BF16)
        ucol = group_rows(ut_ref, gl)
        y = _mm(m_ref[gl], ucol) + _mm_nt(q_ref[gl], ht)
        gt_ref[:, S5_CH * gl:S5_CH * (gl + 1), :] = _gelu(y).astype(BF16).reshape(S5_T, S5_CH, cols)


def _ssm(ut, utc, p_op, m_op, q_op, a_op, n_lat, n_ctx):
    gps = GROUPS_PER_STEP
    cols = ut.shape[2]
    ch = gps * S5_CH
    op = pl.BlockSpec((gps, S5_ROWS, S5_ROWS), lambda i: (i, 0, 0))
    return pl.pallas_call(
        functools.partial(_ssm_kernel, n_lat, n_ctx),
        grid=(S5_GROUPS // gps,),
        in_specs=[pl.BlockSpec((S5_T, ch, cols), lambda i: (0, i, 0)),
                  pl.BlockSpec((S5_T, ch, CTX_LANES), lambda i: (0, i, 0)),
                  op, op, op,
                  pl.BlockSpec((gps, 2 * S5_LANES), lambda i: (i, 0))],
        out_specs=pl.BlockSpec((S5_T, ch, cols), lambda i: (0, i, 0)),
        out_shape=jax.ShapeDtypeStruct(ut.shape, BF16),
        scratch_shapes=[pltpu.VMEM((2, cols * gps, S5_LANES), F32),
                        pltpu.VMEM((2, CTX_LANES * gps, S5_LANES), F32),
                        pltpu.VMEM((2, cols * gps, S5_LANES), F32)],
        compiler_params=_params("arbitrary"),
        name="ssm",
    )(ut, utc, p_op, m_op, q_op, a_op)


def _glu_kernel(gt_ref, wglu_ref, bglu_ref, yb_ref, t_sc):
    nc = gt_ref.shape[2]
    n_slabs = W_B // LANES
    for half in range(S5_T // SUBLANES):
        g = jnp.concatenate([gt_ref[SUBLANES * half + jl] for jl in range(SUBLANES)], axis=1)
        a = _mm(wglu_ref[...], g) + bglu_ref[...]
        ybt = (g.astype(F32) * _sigmoid(a)).T
        for jl in range(SUBLANES):
            for s in range(n_slabs):
                t_sc[s, pl.ds(SUBLANES * half + jl, nc, stride=S5_T), :] = (
                    ybt[nc * jl:nc * (jl + 1), LANES * s:LANES * (s + 1)])
    yb_ref[0] = jnp.concatenate([t_sc[s] for s in range(n_slabs)], axis=1).astype(BF16)


def _glu(gt, w_glu_t, b_glu_col, bsz):
    nc = CHUNKS_PER_STEP
    tiles = gt.shape[2] // nc
    per_batch = tiles // bsz
    rows = nc * S5_T
    return pl.pallas_call(
        _glu_kernel,
        grid=(tiles,),
        in_specs=[pl.BlockSpec((S5_T, W_B, nc), lambda i: (0, 0, i)),
                  pl.BlockSpec((W_B, W_B), lambda i: (0, 0), pipeline_mode=pl.Buffered(1)),
                  pl.BlockSpec((W_B, 1), lambda i: (0, 0))],
        out_specs=pl.BlockSpec((1, rows, W_B), lambda i: (i // per_batch, i % per_batch, 0)),
        out_shape=jax.ShapeDtypeStruct((bsz, per_batch * rows, W_B), BF16),
        scratch_shapes=[pltpu.VMEM((W_B // LANES, rows, LANES), F32)],
        compiler_params=_params("arbitrary"),
        name="glu",
    )(gt, w_glu_t, b_glu_col)


def _out_kernel(x_ref, ya_ref, yb_ref, zb_ref, gate_ref, wout_ref, lng_ref, lnb_ref, o_ref):
    yb = (yb_ref[0].astype(F32) * zb_ref[0].astype(F32)).astype(BF16)
    o = _mm(ya_ref[0], wout_ref[:W_A, :]) + _mm(yb, wout_ref[W_A:, :])
    r = ALPHA * x_ref[0] + gate_ref[0] * o
    o_ref[0] = _layer_norm(r) * lng_ref[...] + lnb_ref[...]


def _out(x, ya, yb, zb, gate, w_out, ln_g, ln_b, tm=512):
    bsz, seq, _ = x.shape
    const2 = lambda b, i: (0, 0)
    tok = lambda width: pl.BlockSpec((1, tm, width), lambda b, i: (b, i, 0))
    return pl.pallas_call(
        _out_kernel,
        grid=(bsz, seq // tm),
        in_specs=[tok(D_MODEL), tok(W_A), tok(W_B), tok(W_B),
                  pl.BlockSpec((1, 1, D_MODEL), lambda b, i: (b, 0, 0)),
                  pl.BlockSpec((W_A + W_B, D_MODEL), const2, pipeline_mode=pl.Buffered(1)),
                  pl.BlockSpec((1, D_MODEL), const2),
                  pl.BlockSpec((1, D_MODEL), const2)],
        out_specs=tok(D_MODEL),
        out_shape=jax.ShapeDtypeStruct(x.shape, F32),
        compiler_params=_params("arbitrary", "arbitrary"),
        name="out",
    )(x, ya, yb, zb, gate, w_out, ln_g, ln_b)


def kernel(x, c, ctx, c_ctx, w_ada, b_ada, w_in, sgu_ln_g, sgu_ln_b, w_spatial, b_spatial,
           s5_lam_re, s5_lam_im, s5_log_step, s5_b_re, s5_b_im, s5_c_re, s5_c_im,
           s5_d, w_glu, b_glu, w_out, ln_g, ln_b):
    depth = w_ada.shape[0]
    assert depth == 1, "context update between layers is not implemented"
    bsz, seq, d = x.shape
    n_lat = seq // S5_T
    n_ctx = ctx.shape[1] // S5_T
    assert d == D_MODEL and bsz + 1 <= 8
    assert seq % (S5_T * CHUNKS_PER_STEP) == 0 and ctx.shape[1] % S5_T == 0
    assert bsz * n_ctx <= CTX_LANES
    i = 0

    c_rows = jnp.concatenate([c, c_ctx[None], jnp.zeros((8 - bsz - 1, d), F32)], axis=0)
    mods = _mods(c_rows, w_ada[i], b_ada[i][None])
    shift_x = mods[:bsz, None, :d]
    scale_x = mods[:bsz, None, d:2 * d]
    gate_x = mods[:bsz, None, 2 * d:]
    shift_c = mods[bsz:bsz + 1, :d]
    scale_c = mods[bsz:bsz + 1, d:2 * d]

    w_in_bf = w_in[i].astype(BF16)

    p_op, m_op, q_op, a_op = _s5_ops(s5_lam_re[i], s5_lam_im[i], s5_log_step[i],
                                     s5_b_re[i], s5_b_im[i], s5_c_re[i], s5_c_im[i], s5_d[i])

    b_sp = jnp.repeat(b_spatial[i].T, HEAD_DIM_A, axis=1)
    ya, zb = _branch_a(x, shift_x, scale_x, w_in_bf, sgu_ln_g[i][None], sgu_ln_b[i][None],
                       w_spatial[i].astype(BF16), b_sp)

    ut = _proj_b(x, shift_x, scale_x, w_in_bf)
    utc = _proj_ctx(ctx, shift_c, scale_c, w_in_bf)
    gt = _ssm(ut, utc, p_op, m_op, q_op, a_op.reshape(S5_GROUPS, 2 * S5_LANES), n_lat, n_ctx)
    yb = _glu(gt, w_glu[i].T.astype(BF16), b_glu[i][:, None], bsz)

    return _out(x, ya, yb, zb, gate_x, w_out[i].astype(BF16), ln_g[i][None], ln_b[i][None])
```
